```python
import jax, jax.numpy as jnp
from jax import lax
import numpy as np

D_MODEL = 1024
BATCH = 2
SEQ = 8192
DEPTH = 1

D_MIX = D_MODEL
D_LRU = D_MIX // 2
LRU_BLOCKS = 8
LRU_BLOCK = D_LRU // LRU_BLOCKS
CONV_WIDTH = 4
LRU_C = 8.0
N_HEADS = 8
N_KV_HEADS = 2
GQA_GROUP = N_HEADS // N_KV_HEADS
HEAD_DIM = (D_MIX - D_LRU) // N_HEADS
D_ATTN = N_HEADS * HEAD_DIM
WINDOW = 128
BLOCK_Q = 128
D_FF = 2816
RMS_EPS = 1e-6
D_IN = 2 * D_LRU + (N_HEADS + 2 * N_KV_HEADS) * HEAD_DIM
MASK_VALUE = -1e30

kernel_name = "hybrid_rglru_swa_sink_macaron"


def rms_norm(x, g):
    xf = x.astype(jnp.float32)
    y = xf * lax.rsqrt(jnp.mean(xf * xf, axis=-1, keepdims=True) + RMS_EPS)
    return (y * g.astype(jnp.float32)).astype(x.dtype)


def swiglu(x, w_gu, w_down):
    gate, up = jnp.split(x @ w_gu, 2, axis=-1)
    return (jax.nn.silu(gate) * up) @ w_down


def causal_depthwise_conv(x, w, b):
    c = x.shape[-1]
    rhs = w.astype(x.dtype)[:, None, :]
    y = lax.conv_general_dilated(x, rhs, window_strides=(1,), padding=[(CONV_WIDTH - 1, 0)],
                                 dimension_numbers=("NWC", "WIO", "NWC"), feature_group_count=c)
    return y + b.astype(x.dtype)


def block_diag(x, w, b):
    bsz, s, _ = x.shape
    xh = x.reshape(bsz, s, LRU_BLOCKS, LRU_BLOCK)
    y = jnp.einsum("bshi,hij->bshj", xh, w).reshape(bsz, s, D_LRU)
    return y + b


def rg_lru(x_in, conv_w, conv_b, w_rg, b_rg, w_ig, b_ig, lam):
    xc = causal_depthwise_conv(x_in, conv_w, conv_b)
    r = jax.nn.sigmoid(block_diag(xc, w_rg, b_rg).astype(jnp.float32))
    i = jax.nn.sigmoid(block_diag(xc, w_ig, b_ig).astype(jnp.float32))
    log_a = LRU_C * r * jax.nn.log_sigmoid(lam.astype(jnp.float32))
    a = jnp.exp(log_a)
    mult = jnp.sqrt(-jnp.expm1(2.0 * log_a))
    u = mult * (i * xc.astype(jnp.float32))

    def combine(left, right):
        a_l, u_l = left
        a_r, u_r = right
        return a_l * a_r, a_r * u_l + u_r

    _, h = lax.associative_scan(combine, (a, u), axis=1)
    return h.astype(x_in.dtype)


def sliding_window_sink_attention(q, k, v, sinks):
    bsz, s = q.shape[0], q.shape[1]
    nb = s // BLOCK_Q
    qb = q.reshape(bsz, nb, BLOCK_Q, N_KV_HEADS, GQA_GROUP, HEAD_DIM)
    kb = k.reshape(bsz, nb, BLOCK_Q, N_KV_HEADS, HEAD_DIM)
    vb = v.reshape(bsz, nb, BLOCK_Q, N_KV_HEADS, HEAD_DIM)
    k_band = jnp.concatenate([jnp.concatenate([jnp.zeros_like(kb[:, :1]), kb[:, :-1]], axis=1), kb], axis=2)
    v_band = jnp.concatenate([jnp.concatenate([jnp.zeros_like(vb[:, :1]), vb[:, :-1]], axis=1), vb], axis=2)
    scale = HEAD_DIM ** -0.5
    scores = jnp.einsum("bnqkgd,bnjkd->bnkgqj", qb, k_band).astype(jnp.float32) * scale
    qi = jnp.arange(BLOCK_Q)[:, None]
    kj = jnp.arange(2 * BLOCK_Q)[None, :]
    rel = qi + BLOCK_Q - kj
    band = (rel >= 0) & (rel < WINDOW)
    blk = jnp.arange(nb)[:, None, None]
    mask = band[None] & ((blk > 0) | (kj >= BLOCK_Q)[None])
    scores = jnp.where(mask[None, :, None, None], scores, MASK_VALUE)
    sink = sinks.astype(jnp.float32).reshape(1, 1, N_KV_HEADS, GQA_GROUP, 1, 1)
    m = jnp.maximum(jnp.max(scores, axis=-1, keepdims=True), sink)
    p = jnp.exp(scores - m)
    denom = jnp.sum(p, axis=-1, keepdims=True) + jnp.exp(sink - m)
    p = (p / denom).astype(v.dtype)
    out = jnp.einsum("bnkgqj,bnjkd->bnqkgd", p, v_band)
    return out.reshape(bsz, s, D_ATTN)


def hybrid_mixer(h, w_in, conv_w, conv_b, w_rg, b_rg, w_ig, b_ig, lam, sinks, g_lru_out, g_attn_out, w_o):
    bsz, s, _ = h.shape
    proj = h @ w_in
    o1 = D_LRU
    o2 = o1 + D_LRU
    o3 = o2 + D_ATTN
    o4 = o3 + N_KV_HEADS * HEAD_DIM
    x_lru = proj[..., :o1]
    gate_lru = proj[..., o1:o2]
    q = proj[..., o2:o3].reshape(bsz, s, N_HEADS, HEAD_DIM)
    k = proj[..., o3:o4].reshape(bsz, s, N_KV_HEADS, HEAD_DIM)
    v = proj[..., o4:].reshape(bsz, s, N_KV_HEADS, HEAD_DIM)
    y_lru = rg_lru(x_lru, conv_w, conv_b, w_rg, b_rg, w_ig, b_ig, lam) * jax.nn.gelu(gate_lru)
    y_attn = sliding_window_sink_attention(q, k, v, sinks)
    y = jnp.concatenate([rms_norm(y_lru, g_lru_out), rms_norm(y_attn, g_attn_out)], axis=-1)
    return y @ w_o


def setup_inputs(seed: int = 0) -> dict:
    key = jax.random.key(seed)
    ks = jax.random.split(key, 32)
    f32 = jnp.float32

    def nrm(k, shape, scale):
        return jax.random.normal(k, shape, f32) * scale

    def gain(k, n):
        return 1.0 + 0.05 * jax.random.normal(k, (DEPTH, n), f32)

    u = jax.random.uniform(ks[13], (DEPTH, D_LRU), f32, 0.9, 0.999)
    return {
        "x": jax.random.normal(ks[0], (BATCH, SEQ, D_MODEL), f32),
        "ffn1_pre_g": gain(ks[1], D_MODEL),
        "ffn1_w_gu": nrm(ks[2], (DEPTH, D_MODEL, 2 * D_FF), D_MODEL ** -0.5),
        "ffn1_w_down": nrm(ks[3], (DEPTH, D_FF, D_MODEL), D_FF ** -0.5),
        "ffn1_post_g": gain(ks[4], D_MODEL),
        "mix_pre_g": gain(ks[5], D_MODEL),
        "w_in": nrm(ks[6], (DEPTH, D_MODEL, D_IN), D_MODEL ** -0.5),
        "conv_w": nrm(ks[7], (DEPTH, CONV_WIDTH, D_LRU), CONV_WIDTH ** -0.5),
        "conv_b": nrm(ks[8], (DEPTH, D_LRU), 0.01),
        "w_rg": nrm(ks[9], (DEPTH, LRU_BLOCKS, LRU_BLOCK, LRU_BLOCK), LRU_BLOCK ** -0.5),
        "b_rg": nrm(ks[10], (DEPTH, D_LRU), 0.01),
        "w_ig": nrm(ks[11], (DEPTH, LRU_BLOCKS, LRU_BLOCK, LRU_BLOCK), LRU_BLOCK ** -0.5),
        "b_ig": nrm(ks[12], (DEPTH, D_LRU), 0.01),
        "lru_lambda": jnp.log(u) - jnp.log1p(-u),
        "sinks": nrm(ks[14], (DEPTH, N_HEADS), 1.0),
        "g_lru_out": gain(ks[15], D_LRU),
        "g_attn_out": gain(ks[16], D_ATTN),
        "w_o": nrm(ks[17], (DEPTH, D_MIX, D_MODEL), D_MIX ** -0.5),
        "mix_post_g": gain(ks[18], D_MODEL),
        "ffn2_pre_g": gain(ks[19], D_MODEL),
        "ffn2_w_gu": nrm(ks[20], (DEPTH, D_MODEL, 2 * D_FF), D_MODEL ** -0.5),
        "ffn2_w_down": nrm(ks[21], (DEPTH, D_FF, D_MODEL), D_FF ** -0.5),
        "ffn2_post_g": gain(ks[22], D_MODEL),
    }


def reference(x, ffn1_pre_g, ffn1_w_gu, ffn1_w_down, ffn1_post_g, mix_pre_g, w_in, conv_w, conv_b,
              w_rg, b_rg, w_ig, b_ig, lru_lambda, sinks, g_lru_out, g_attn_out, w_o, mix_post_g,
              ffn2_pre_g, ffn2_w_gu, ffn2_w_down, ffn2_post_g):
    for l in range(DEPTH):
        x = x + 0.5 * rms_norm(swiglu(rms_norm(x, ffn1_pre_g[l]), ffn1_w_gu[l], ffn1_w_down[l]), ffn1_post_g[l])
        y = hybrid_mixer(rms_norm(x, mix_pre_g[l]), w_in[l], conv_w[l], conv_b[l], w_rg[l], b_rg[l],
                         w_ig[l], b_ig[l], lru_lambda[l], sinks[l], g_lru_out[l], g_attn_out[l], w_o[l])
        x = x + rms_norm(y, mix_post_g[l])
        x = x + 0.5 * rms_norm(swiglu(rms_norm(x, ffn2_pre_g[l]), ffn2_w_gu[l], ffn2_w_down[l]), ffn2_post_g[l])
    return x
```

```python
import functools

import jax
import jax.numpy as jnp
from jax import lax
from jax.experimental import pallas as pl
from jax.experimental.pallas import tpu as pltpu

D_MODEL = 1024
D_LRU = 512
LRU_BLOCKS = 8
LRU_BLOCK = 64
CONV_WIDTH = 4
LRU_C = 8.0
N_HEADS = 8
N_KV_HEADS = 2
GQA_GROUP = 4
HEAD_DIM = 64
D_ATTN = 512
WINDOW = 128
BLOCK_Q = 128
D_FF = 2816
RMS_EPS = 1e-6
MASK_VALUE = -1e30

O_GATE = D_LRU
O_Q = 2 * D_LRU
O_K = O_Q + D_ATTN
O_V = O_K + N_KV_HEADS * HEAD_DIM
D_IN = O_V + N_KV_HEADS * HEAD_DIM

SUBLANES = 8
LRU_TILE = 256

FFN_TM = 512
FFN_FC = 256
MIX_T = 256
VMEM_LIMIT = 56 * 1024 * 1024

F32 = jnp.float32
BF16 = jnp.bfloat16


def _rms(x, g):
    ms = jnp.mean(x * x, axis=-1, keepdims=True)
    return (x * lax.rsqrt(ms + RMS_EPS)) * g


def _ffn_kernel(x_ref, pre_g_ref, w_gu_ref, w_down_ref, post_g_ref, o_ref, acc_ref):
    x = x_ref[...]
    h = _rms(x, pre_g_ref[...]).astype(BF16)
    for c in range(D_FF // FFN_FC):
        lo = c * FFN_FC
        g = jnp.dot(h, w_gu_ref[:, lo:lo + FFN_FC], preferred_element_type=F32)
        u = jnp.dot(h, w_gu_ref[:, D_FF + lo:D_FF + lo + FFN_FC], preferred_element_type=F32)
        a = ((g * jax.nn.sigmoid(g)) * u).astype(BF16)
        d = jnp.dot(a, w_down_ref[lo:lo + FFN_FC, :], preferred_element_type=F32)
        if c == 0:
            acc_ref[...] = d
        else:
            acc_ref[...] += d
    o_ref[...] = x + 0.5 * _rms(acc_ref[...], post_g_ref[...])


def _ffn(x2d, pre_g, w_gu, w_down, post_g):
    n = x2d.shape[0]
    const = lambda i: (0, 0)
    return pl.pallas_call(
        _ffn_kernel,
        name="ffn",
        grid=(n // FFN_TM,),
        in_specs=[
            pl.BlockSpec((FFN_TM, D_MODEL), lambda i: (i, 0)),
            pl.BlockSpec((1, D_MODEL), const),
            pl.BlockSpec((D_MODEL, 2 * D_FF), const, pipeline_mode=pl.Buffered(1)),
            pl.BlockSpec((D_FF, D_MODEL), const, pipeline_mode=pl.Buffered(1)),
            pl.BlockSpec((1, D_MODEL), const),
        ],
        out_specs=pl.BlockSpec((FFN_TM, D_MODEL), lambda i: (i, 0)),
        out_shape=jax.ShapeDtypeStruct((n, D_MODEL), F32),
        scratch_shapes=[pltpu.VMEM((FFN_TM, D_MODEL), F32)],
        compiler_params=pltpu.CompilerParams(
            dimension_semantics=("arbitrary",), vmem_limit_bytes=VMEM_LIMIT),
    )(x2d, pre_g, w_gu, w_down, post_g)


def _log_sigmoid(x):
    return jnp.minimum(x, 0.0) - jnp.log1p(jnp.exp(-jnp.abs(x)))


def _mixer_kernel(sinks_ref, x_ref, pre_g_ref, w_in_ref, conv_w_ref, conv_b_ref,
                  w_rg_ref, b_rg_ref, w_ig_ref, b_ig_ref, lam_ref,
                  g_lru_ref, g_attn_ref, w_o_ref, post_g_ref,
                  o_ref,
                  xbuf_ref, a_ref, h_ref, hcarry_ref, kbuf_ref, vbuf_ref, yattn_ref):
    t = pl.program_id(1)
    T = MIX_T

    @pl.when(t == 0)
    def _():
        xbuf_ref[0:SUBLANES, :] = jnp.zeros((SUBLANES, D_LRU), F32)
        hcarry_ref[...] = jnp.zeros_like(hcarry_ref)
        kbuf_ref[0:BLOCK_Q, :] = jnp.zeros((BLOCK_Q, N_KV_HEADS * HEAD_DIM), BF16)
        vbuf_ref[0:BLOCK_Q, :] = jnp.zeros((BLOCK_Q, N_KV_HEADS * HEAD_DIM), BF16)

    x = x_ref[...]
    hn = _rms(x, pre_g_ref[...]).astype(BF16)
    proj = jnp.dot(hn, w_in_ref[...], preferred_element_type=F32)

    xl = proj[:, 0:D_LRU]
    gate = proj[:, O_GATE:O_GATE + D_LRU]
    xbuf_ref[SUBLANES:SUBLANES + T, :] = xl
    cw = conv_w_ref[...]
    xc = conv_b_ref[...] + cw[CONV_WIDTH - 1:CONV_WIDTH, :] * xl
    for j in range(1, CONV_WIDTH):
        k = CONV_WIDTH - 1 - j
        xc = xc + cw[k:k + 1, :] * xbuf_ref[SUBLANES - j:SUBLANES - j + T, :]
    xbuf_ref[0:SUBLANES, :] = xl[T - SUBLANES:T, :]

    xcb = xc.astype(BF16)
    r_parts, i_parts = [], []
    for b in range(D_LRU // LRU_TILE):
        xs = xcb[:, b * LRU_TILE:(b + 1) * LRU_TILE]
        r_parts.append(jnp.dot(xs, w_rg_ref[b], preferred_element_type=F32))
        i_parts.append(jnp.dot(xs, w_ig_ref[b], preferred_element_type=F32))
    r = jax.nn.sigmoid(jnp.concatenate(r_parts, axis=-1) + b_rg_ref[...])
    ig = jax.nn.sigmoid(jnp.concatenate(i_parts, axis=-1) + b_ig_ref[...])
    log_a = (LRU_C * r) * _log_sigmoid(lam_ref[...])
    a = jnp.exp(log_a)
    mult = jnp.sqrt(-jnp.tanh(log_a) * (a * a + 1.0))
    u = mult * (ig * xc)

    row = lax.broadcasted_iota(jnp.int32, (T, D_LRU), 0) & (SUBLANES - 1)
    A, H = a, u
    d = 1
    while d < SUBLANES:
        m = row >= d
        A_s = jnp.where(m, pltpu.roll(A, d, 0), 1.0)
        H_s = jnp.where(m, pltpu.roll(H, d, 0), 0.0)
        H = H + A * H_s
        A = A * A_s
        d *= 2
    a_ref[...] = A
    h_ref[...] = H
    carry = hcarry_ref[0:1, :]
    for g in range(T // SUBLANES):
        rows = slice(g * SUBLANES, (g + 1) * SUBLANES)
        hg = h_ref[rows, :] + a_ref[rows, :] * carry
        h_ref[rows, :] = hg
        carry = hg[SUBLANES - 1:SUBLANES, :]
    hcarry_ref[0:1, :] = carry

    y_lru = h_ref[...] * jax.nn.gelu(gate, approximate=True)
    yl = _rms(y_lru, g_lru_ref[...])

    scale = HEAD_DIM ** -0.5
    q = (proj[:, O_Q:O_Q + D_ATTN] * scale).astype(BF16)
    kbuf_ref[BLOCK_Q:BLOCK_Q + T, :] = proj[:, O_K:O_K + N_KV_HEADS * HEAD_DIM].astype(BF16)
    vbuf_ref[BLOCK_Q:BLOCK_Q + T, :] = proj[:, O_V:O_V + N_KV_HEADS * HEAD_DIM].astype(BF16)

    rows4 = GQA_GROUP * BLOCK_Q
    qi = lax.broadcasted_iota(jnp.int32, (rows4, 2 * BLOCK_Q), 0) & (BLOCK_Q - 1)
    kj = lax.broadcasted_iota(jnp.int32, (rows4, 2 * BLOCK_Q), 1)
    rel = qi + BLOCK_Q - kj
    band = (rel >= 0) & (rel < WINDOW)
    band0 = band & (kj >= jnp.where(t > 0, 0, BLOCK_Q))
    head_of_row = lax.broadcasted_iota(jnp.int32, (rows4, 1), 0) // BLOCK_Q

    for qb in range(T // BLOCK_Q):
        qblk = q[qb * BLOCK_Q:(qb + 1) * BLOCK_Q, :]
        kband = kbuf_ref[qb * BLOCK_Q:(qb + 2) * BLOCK_Q, :]
        vband = vbuf_ref[qb * BLOCK_Q:(qb + 2) * BLOCK_Q, :]
        mask = band0 if qb == 0 else band
        for g in range(N_KV_HEADS):
            heads = [g * GQA_GROUP + hh for hh in range(GQA_GROUP)]
            q4 = jnp.concatenate([qblk[:, h * HEAD_DIM:(h + 1) * HEAD_DIM] for h in heads], axis=0)
            kg = kband[:, g * HEAD_DIM:(g + 1) * HEAD_DIM]
            vg = vband[:, g * HEAD_DIM:(g + 1) * HEAD_DIM]
            s = lax.dot_general(q4, kg, (((1,), (1,)), ((), ())), preferred_element_type=F32)
            s = jnp.where(mask, s, MASK_VALUE)
            sink = jnp.full((rows4, 1), sinks_ref[heads[-1]], F32)
            for hh in range(GQA_GROUP - 1):
                sink = jnp.where(head_of_row == hh, sinks_ref[heads[hh]], sink)
            mx = jnp.maximum(jnp.max(s, axis=-1, keepdims=True), sink)
            p = jnp.exp(s - mx)
            denom = jnp.sum(p, axis=-1, keepdims=True) + jnp.exp(sink - mx)
            o = jnp.dot(p.astype(BF16), vg, preferred_element_type=F32) / denom
            for hh, h in enumerate(heads):
                yattn_ref[qb * BLOCK_Q:(qb + 1) * BLOCK_Q, h * HEAD_DIM:(h + 1) * HEAD_DIM] = (
                    o[hh * BLOCK_Q:(hh + 1) * BLOCK_Q, :])
    kbuf_ref[0:BLOCK_Q, :] = kbuf_ref[T:T + BLOCK_Q, :]
    vbuf_ref[0:BLOCK_Q, :] = vbuf_ref[T:T + BLOCK_Q, :]

    ya = _rms(yattn_ref[...], g_attn_ref[...])
    y = jnp.concatenate([yl, ya], axis=-1).astype(BF16)
    out = jnp.dot(y, w_o_ref[...], preferred_element_type=F32)
    o_ref[...] = x + _rms(out, post_g_ref[...])


def _mixer(x, sinks, pre_g, w_in, conv_w, conv_b, w_rg, b_rg, w_ig, b_ig, lam,
           g_lru, g_attn, w_o, post_g):
    bsz, s, _ = x.shape
    T = MIX_T
    c2 = lambda b, t, *_: (0, 0)
    c3 = lambda b, t, *_: (0, 0, 0)
    n_kv = N_KV_HEADS * HEAD_DIM
    grid_spec = pltpu.PrefetchScalarGridSpec(
        num_scalar_prefetch=1,
        grid=(bsz, s // T),
        in_specs=[
            pl.BlockSpec((None, T, D_MODEL), lambda b, t, *_: (b, t, 0)),
            pl.BlockSpec((1, D_MODEL), c2),
            pl.BlockSpec((D_MODEL, D_IN), c2),
            pl.BlockSpec((CONV_WIDTH, D_LRU), c2),
            pl.BlockSpec((1, D_LRU), c2),
            pl.BlockSpec((D_LRU // LRU_TILE, LRU_TILE, LRU_TILE), c3),
            pl.BlockSpec((1, D_LRU), c2),
            pl.BlockSpec((D_LRU // LRU_TILE, LRU_TILE, LRU_TILE), c3),
            pl.BlockSpec((1, D_LRU), c2),
            pl.BlockSpec((1, D_LRU), c2),
            pl.BlockSpec((1, D_LRU), c2),
            pl.BlockSpec((1, D_ATTN), c2),
            pl.BlockSpec((D_MODEL, D_MODEL), c2),
            pl.BlockSpec((1, D_MODEL), c2),
        ],
        out_specs=pl.BlockSpec((None, T, D_MODEL), lambda b, t, *_: (b, t, 0)),
        scratch_shapes=[
            pltpu.VMEM((SUBLANES + T, D_LRU), F32),
            pltpu.VMEM((T, D_LRU), F32),
            pltpu.VMEM((T, D_LRU), F32),
            pltpu.VMEM((SUBLANES, D_LRU), F32),
            pltpu.VMEM((BLOCK_Q + T, n_kv), BF16),
            pltpu.VMEM((BLOCK_Q + T, n_kv), BF16),
            pltpu.VMEM((T, D_ATTN), F32),
        ],
    )
    return pl.pallas_call(
        _mixer_kernel,
        name="mixer",
        grid_spec=grid_spec,
        out_shape=jax.ShapeDtypeStruct(x.shape, F32),
        compiler_params=pltpu.CompilerParams(
            dimension_semantics=("arbitrary", "arbitrary"), vmem_limit_bytes=VMEM_LIMIT),
    )(sinks, x, pre_g, w_in, conv_w, conv_b, w_rg, b_rg, w_ig, b_ig, lam,
      g_lru, g_attn, w_o, post_g)


def _block_diag_tiles(w):
    per = LRU_TILE // LRU_BLOCK
    w4 = w.reshape(D_LRU // LRU_TILE, per, LRU_BLOCK, LRU_BLOCK)
    eye = jnp.eye(per, dtype=w.dtype)
    t = jnp.einsum("tpij,pq->tpiqj", w4, eye)
    return t.reshape(D_LRU // LRU_TILE, LRU_TILE, LRU_TILE)


def kernel(x, ffn1_pre_g, ffn1_w_gu, ffn1_w_down, ffn1_post_g, mix_pre_g, w_in, conv_w, conv_b, w_rg, b_rg, w_ig, b_ig, lru_lambda, sinks, g_lru_out, g_attn_out, w_o, mix_post_g, ffn2_pre_g, ffn2_w_gu, ffn2_w_down, ffn2_post_g):
    bsz, s, d = x.shape
    depth = ffn1_pre_g.shape[0]
    for l in range(depth):
        x2 = _ffn(x.reshape(bsz * s, d), ffn1_pre_g[l][None], ffn1_w_gu[l].astype(BF16),
                  ffn1_w_down[l].astype(BF16), ffn1_post_g[l][None])
        x = _mixer(x2.reshape(bsz, s, d), sinks[l], mix_pre_g[l][None], w_in[l].astype(BF16),
                   conv_w[l], conv_b[l][None],
                   _block_diag_tiles(w_rg[l]).astype(BF16), b_rg[l][None],
                   _block_diag_tiles(w_ig[l]).astype(BF16), b_ig[l][None],
                   lru_lambda[l][None], g_lru_out[l][None], g_attn_out[l][None],
                   w_o[l].astype(BF16), mix_post_g[l][None])
        x2 = _ffn(x.reshape(bsz * s, d), ffn2_pre_g[l][None], ffn2_w_gu[l].astype(BF16),
                  ffn2_w_down[l].astype(BF16), ffn2_post_g[l][None])
        x = x2.reshape(bsz, s, d)
    return x
```

```python
import jax
import jax.numpy as jnp
from jax import lax
from jax.experimental import pallas as pl
from jax.experimental.pallas import tpu as pltpu

D_MODEL = 1024
D_LRU = 512
LRU_BLOCKS = 8
LRU_BLOCK = 64
CONV_WIDTH = 4
LRU_C = 8.0
N_HEADS = 8
N_KV_HEADS = 2
GQA_GROUP = 4
HEAD_DIM = 64
D_ATTN = 512
WINDOW = 128
BLOCK_Q = 128
D_FF = 2816
RMS_EPS = 1e-6
MASK_VALUE = -1e30

O_GATE = D_LRU
O_Q = 2 * D_LRU
O_K = O_Q + D_ATTN
O_V = O_K + N_KV_HEADS * HEAD_DIM
D_IN = O_V + N_KV_HEADS * HEAD_DIM

LANES = 128
SUBLANES = 8
LRU_TILE = 256
LRU_SLABS = D_LRU // LANES

FFN_TM = 512
FFN_FC = 256
MIX_T = 256
MIX_NG = MIX_T // SUBLANES
VMEM_LIMIT = 56 * 1024 * 1024

F32 = jnp.float32
BF16 = jnp.bfloat16


def _rms(x, g):
    ms = jnp.mean(x * x, axis=-1, keepdims=True)
    return (x * lax.rsqrt(ms + RMS_EPS)) * g


def _ffn_kernel(x_ref, pre_g_ref, w_gu_ref, w_down_ref, post_g_ref, o_ref, acc_ref):
    x = x_ref[...]
    h = _rms(x, pre_g_ref[...]).astype(BF16)
    for c in range(D_FF // FFN_FC):
        lo = c * FFN_FC
        g = jnp.dot(h, w_gu_ref[:, lo:lo + FFN_FC], preferred_element_type=F32)
        u = jnp.dot(h, w_gu_ref[:, D_FF + lo:D_FF + lo + FFN_FC], preferred_element_type=F32)
        a = ((g * jax.nn.sigmoid(g)) * u).astype(BF16)
        d = jnp.dot(a, w_down_ref[lo:lo + FFN_FC, :], preferred_element_type=F32)
        if c == 0:
            acc_ref[...] = d
        else:
            acc_ref[...] += d
    o_ref[...] = x + 0.5 * _rms(acc_ref[...], post_g_ref[...])


def _ffn(x2d, pre_g, w_gu, w_down, post_g):
    n = x2d.shape[0]
    const = lambda i: (0, 0)
    return pl.pallas_call(
        _ffn_kernel,
        name="ffn",
        grid=(n // FFN_TM,),
        in_specs=[
            pl.BlockSpec((FFN_TM, D_MODEL), lambda i: (i, 0)),
            pl.BlockSpec((1, D_MODEL), const),
            pl.BlockSpec((D_MODEL, 2 * D_FF), const, pipeline_mode=pl.Buffered(1)),
            pl.BlockSpec((D_FF, D_MODEL), const, pipeline_mode=pl.Buffered(1)),
            pl.BlockSpec((1, D_MODEL), const),
        ],
        out_specs=pl.BlockSpec((FFN_TM, D_MODEL), lambda i: (i, 0)),
        out_shape=jax.ShapeDtypeStruct((n, D_MODEL), F32),
        scratch_shapes=[pltpu.VMEM((FFN_TM, D_MODEL), F32)],
        compiler_params=pltpu.CompilerParams(
            dimension_semantics=("arbitrary",), vmem_limit_bytes=VMEM_LIMIT),
    )(x2d, pre_g, w_gu, w_down, post_g)


def _log_sigmoid(x):
    return jnp.minimum(x, 0.0) - jnp.log1p(jnp.exp(-jnp.abs(x)))


def _mixer_kernel(sinks_ref, x_ref, pre_g_ref, w_in_ref, conv_w_ref, conv_b_ref,
                  w_rg_ref, b_rg_ref, w_ig_ref, b_ig_ref, lam_ref,
                  g_lru_ref, g_attn_ref, w_o_ref, post_g_ref,
                  o_ref,
                  xl_ref, xtail_ref, h_ref, ga_ref, gh_ref, c_ref, hcarry_ref, kbuf_ref, vaug_ref,
                  yattn_ref, cap_ref, sink_ref):
    t = pl.program_id(1)
    T, NG = MIX_T, MIX_NG
    n_kv = N_KV_HEADS * HEAD_DIM
    rows4 = GQA_GROUP * BLOCK_Q
    cur = lax.rem(t, 2)
    prev = 1 - cur

    @pl.when(t == 0)
    def _():
        xtail_ref[...] = jnp.zeros_like(xtail_ref)
        hcarry_ref[...] = jnp.zeros_like(hcarry_ref)
        kbuf_ref[1, T - BLOCK_Q:T, :] = jnp.zeros((BLOCK_Q, n_kv), BF16)
        for g in range(N_KV_HEADS):
            vaug_ref[N_KV_HEADS + g, T - BLOCK_Q:T, 0:LANES] = jnp.zeros((BLOCK_Q, LANES), BF16)
        vaug_ref[:, :, LANES:2 * LANES] = jnp.ones((2 * N_KV_HEADS, T, LANES), BF16)
        qi = lax.broadcasted_iota(jnp.int32, (rows4, 2 * BLOCK_Q), 0) & (BLOCK_Q - 1)
        kj = lax.broadcasted_iota(jnp.int32, (rows4, 2 * BLOCK_Q), 1)
        rel = qi + BLOCK_Q - kj
        band = (rel >= 0) & (rel < WINDOW)
        cap_ref[0] = jnp.where(band, jnp.inf, MASK_VALUE)
        cap_ref[1] = jnp.where(band & (kj >= BLOCK_Q), jnp.inf, MASK_VALUE)
        head_of_row = lax.broadcasted_iota(jnp.int32, (rows4, LANES), 0) // BLOCK_Q
        for g in range(N_KV_HEADS):
            sink = jnp.full((rows4, LANES), sinks_ref[g * GQA_GROUP + GQA_GROUP - 1], F32)
            for hh in range(GQA_GROUP - 1):
                sink = jnp.where(head_of_row == hh, sinks_ref[g * GQA_GROUP + hh], sink)
            sink_ref[g] = sink

    @pl.when(t == 1)
    def _():
        cap_ref[1] = cap_ref[0]

    x = x_ref[...]
    hn = _rms(x, pre_g_ref[...]).astype(BF16)
    proj = jnp.dot(hn, w_in_ref[...], preferred_element_type=F32)

    gate = proj[:, O_GATE:O_GATE + D_LRU]
    for j in range(LRU_SLABS):
        xl_ref[j, 0:SUBLANES, :] = xtail_ref[prev * LRU_SLABS + j]
        xl_ref[j, SUBLANES:SUBLANES + T, :] = proj[:, j * LANES:(j + 1) * LANES]
        xtail_ref[cur * LRU_SLABS + j] = proj[T - SUBLANES:T, j * LANES:(j + 1) * LANES]

    def x_rows(r):
        return jnp.concatenate(
            [xl_ref[j, pl.ds(SUBLANES + r, NG, stride=SUBLANES), :] for j in range(LRU_SLABS)],
            axis=-1)

    xs = {r: x_rows(r) for r in range(1 - CONV_WIDTH, SUBLANES)}
    cw = conv_w_ref[...]
    cb = conv_b_ref[...]
    xc_blocks = []
    for r in range(SUBLANES):
        acc = cb + cw[CONV_WIDTH - 1:CONV_WIDTH, :] * xs[r]
        for j in range(1, CONV_WIDTH):
            k = CONV_WIDTH - 1 - j
            acc = acc + cw[k:k + 1, :] * xs[r - j]
        xc_blocks.append(acc)
    xc = jnp.concatenate(xc_blocks, axis=0)

    xcb = xc.astype(BF16)
    r_parts, i_parts = [], []
    for b in range(D_LRU // LRU_TILE):
        xt = xcb[:, b * LRU_TILE:(b + 1) * LRU_TILE]
        r_parts.append(jnp.dot(xt, w_rg_ref[b], preferred_element_type=F32))
        i_parts.append(jnp.dot(xt, w_ig_ref[b], preferred_element_type=F32))
    r = jax.nn.sigmoid(jnp.concatenate(r_parts, axis=-1) + b_rg_ref[...])
    ig = jax.nn.sigmoid(jnp.concatenate(i_parts, axis=-1) + b_ig_ref[...])
    log_a = (LRU_C * r) * _log_sigmoid(lam_ref[...])
    a = jnp.exp(log_a)
    mult = jnp.sqrt(-jnp.tanh(log_a) * (a * a + 1.0))
    u = mult * (ig * xc)

    hs, As = [], []
    for r in range(SUBLANES):
        a_r = a[r * NG:(r + 1) * NG, :]
        u_r = u[r * NG:(r + 1) * NG, :]
        if r == 0:
            h_r, A_r = u_r, a_r
        else:
            h_r = a_r * hs[-1] + u_r
            A_r = a_r * As[-1]
        hs.append(h_r)
        As.append(A_r)
    ga_ref[...] = As[-1]
    gh_ref[...] = hs[-1]
    carry = hcarry_ref[0:1, :]
    for g in range(NG):
        c_ref[g:g + 1, :] = carry
        carry = ga_ref[g:g + 1, :] * carry + gh_ref[g:g + 1, :]
    hcarry_ref[0:1, :] = carry
    c_in = c_ref[...]
    for r in range(SUBLANES):
        hf = hs[r] + As[r] * c_in
        for j in range(LRU_SLABS):
            h_ref[j, pl.ds(r, NG, stride=SUBLANES), :] = hf[:, j * LANES:(j + 1) * LANES]
    h_tok = jnp.concatenate([h_ref[j] for j in range(LRU_SLABS)], axis=-1)

    y_lru = h_tok * jax.nn.gelu(gate, approximate=True)
    yl = _rms(y_lru, g_lru_ref[...])

    scale = HEAD_DIM ** -0.5
    q = (proj[:, O_Q:O_Q + D_ATTN] * scale).astype(BF16)
    kbuf_ref[cur] = proj[:, O_K:O_K + n_kv].astype(BF16)
    v2 = proj[:, O_V:O_V + n_kv]
    v2r = pltpu.roll(v2, HEAD_DIM, 1)
    lane_v = lax.broadcasted_iota(jnp.int32, (T, LANES), 1)
    vaug_ref[cur * N_KV_HEADS, :, 0:LANES] = jnp.where(lane_v < HEAD_DIM, v2, v2r).astype(BF16)
    vaug_ref[cur * N_KV_HEADS + 1, :, 0:LANES] = jnp.where(lane_v < HEAD_DIM, v2r, v2).astype(BF16)

    def band_rows(ref, i_cur, i_prev, qb):
        if qb == 0:
            return jnp.concatenate([ref[i_prev, T - BLOCK_Q:T, :], ref[i_cur, 0:BLOCK_Q, :]], axis=0)
        return ref[i_cur, (qb - 1) * BLOCK_Q:(qb + 1) * BLOCK_Q, :]

    n_qb = T // BLOCK_Q
    s_list, sink_list = [], []
    for qb in range(n_qb):
        qblk = q[qb * BLOCK_Q:(qb + 1) * BLOCK_Q, :]
        kband = band_rows(kbuf_ref, cur, prev, qb)
        cap = cap_ref[1 if qb == 0 else 0]
        for g in range(N_KV_HEADS):
            heads = [g * GQA_GROUP + hh for hh in range(GQA_GROUP)]
            q4 = jnp.concatenate([qblk[:, h * HEAD_DIM:(h + 1) * HEAD_DIM] for h in heads], axis=0)
            kg = kband[:, g * HEAD_DIM:(g + 1) * HEAD_DIM]
            s = lax.dot_general(q4, kg, (((1,), (1,)), ((), ())), preferred_element_type=F32)
            s_list.append(jnp.minimum(s, cap))
            sink_list.append(sink_ref[g])
    s_all = jnp.concatenate(s_list, axis=0)
    sink_all = jnp.concatenate(sink_list, axis=0)
    row_max = jnp.max(s_all, axis=-1, keepdims=True)
    mx = jnp.maximum(jnp.broadcast_to(row_max, sink_all.shape), sink_all)
    p_all = jnp.exp(s_all - jnp.concatenate([mx, mx], axis=-1)).astype(BF16)
    e_sink = jnp.exp(sink_all - mx)
    lane_o = lax.broadcasted_iota(jnp.int32, (BLOCK_Q, LANES), 1)
    for qb in range(n_qb):
        for g in range(N_KV_HEADS):
            i0 = (qb * N_KV_HEADS + g) * rows4
            vband = band_rows(vaug_ref, cur * N_KV_HEADS + g, prev * N_KV_HEADS + g, qb)
            pv = jnp.dot(p_all[i0:i0 + rows4, :], vband,
                         preferred_element_type=F32)
            o2 = pv[:, 0:LANES] / (pv[:, LANES:2 * LANES] + e_sink[i0:i0 + rows4, :])
            for pair in range(GQA_GROUP // 2):
                even = o2[(2 * pair) * BLOCK_Q:(2 * pair + 1) * BLOCK_Q, :]
                odd = o2[(2 * pair + 1) * BLOCK_Q:(2 * pair + 2) * BLOCK_Q, :]
                col = (g * (GQA_GROUP // 2) + pair) * LANES
                yattn_ref[qb * BLOCK_Q:(qb + 1) * BLOCK_Q, col:col + LANES] = (
                    jnp.where(lane_o < HEAD_DIM, even, odd))

    ya = _rms(yattn_ref[...], g_attn_ref[...])
    y = jnp.concatenate([yl, ya], axis=-1).astype(BF16)
    out = jnp.dot(y, w_o_ref[...], preferred_element_type=F32)
    o_ref[...] = x + _rms(out, post_g_ref[...])


def _mixer(x, sinks, pre_g, w_in, conv_w, conv_b, w_rg, b_rg, w_ig, b_ig, lam,
           g_lru, g_attn, w_o, post_g):
    bsz, s, _ = x.shape
    T, NG = MIX_T, MIX_NG
    c2 = lambda b, t, *_: (0, 0)
    c3 = lambda b, t, *_: (0, 0, 0)
    n_kv = N_KV_HEADS * HEAD_DIM
    grid_spec = pltpu.PrefetchScalarGridSpec(
        num_scalar_prefetch=1,
        grid=(bsz, s // T),
        in_specs=[
            pl.BlockSpec((None, T, D_MODEL), lambda b, t, *_: (b, t, 0)),
            pl.BlockSpec((1, D_MODEL), c2),
            pl.BlockSpec((D_MODEL, D_IN), c2),
            pl.BlockSpec((CONV_WIDTH, D_LRU), c2),
            pl.BlockSpec((1, D_LRU), c2),
            pl.BlockSpec((D_LRU // LRU_TILE, LRU_TILE, LRU_TILE), c3),
            pl.BlockSpec((1, D_LRU), c2),
            pl.BlockSpec((D_LRU // LRU_TILE, LRU_TILE, LRU_TILE), c3),
            pl.BlockSpec((1, D_LRU), c2),
            pl.BlockSpec((1, D_LRU), c2),
            pl.BlockSpec((1, D_LRU), c2),
            pl.BlockSpec((1, D_ATTN), c2),
            pl.BlockSpec((D_MODEL, D_MODEL), c2),
            pl.BlockSpec((1, D_MODEL), c2),
        ],
        out_specs=pl.BlockSpec((None, T, D_MODEL), lambda b, t, *_: (b, t, 0)),
        scratch_shapes=[
            pltpu.VMEM((LRU_SLABS, SUBLANES + T, LANES), F32),
            pltpu.VMEM((2 * LRU_SLABS, SUBLANES, LANES), F32),
            pltpu.VMEM((LRU_SLABS, T, LANES), F32),
            pltpu.VMEM((NG, D_LRU), F32),
            pltpu.VMEM((NG, D_LRU), F32),
            pltpu.VMEM((NG, D_LRU), F32),
            pltpu.VMEM((SUBLANES, D_LRU), F32),
            pltpu.VMEM((2, T, n_kv), BF16),
            pltpu.VMEM((2 * N_KV_HEADS, T, 2 * LANES), BF16),
            pltpu.VMEM((T, D_ATTN), F32),
            pltpu.VMEM((2, GQA_GROUP * BLOCK_Q, 2 * BLOCK_Q), F32),
            pltpu.VMEM((N_KV_HEADS, GQA_GROUP * BLOCK_Q, LANES), F32),
        ],
    )
    return pl.pallas_call(
        _mixer_kernel,
        name="mixer",
        grid_spec=grid_spec,
        out_shape=jax.ShapeDtypeStruct(x.shape, F32),
        compiler_params=pltpu.CompilerParams(
            dimension_semantics=("arbitrary", "arbitrary"), vmem_limit_bytes=VMEM_LIMIT),
    )(sinks, x, pre_g, w_in, conv_w, conv_b, w_rg, b_rg, w_ig, b_ig, lam,
      g_lru, g_attn, w_o, post_g)


def _block_diag_tiles(w):
    per = LRU_TILE // LRU_BLOCK
    w4 = w.reshape(D_LRU // LRU_TILE, per, LRU_BLOCK, LRU_BLOCK)
    eye = jnp.eye(per, dtype=w.dtype)
    t = jnp.einsum("tpij,pq->tpiqj", w4, eye)
    return t.reshape(D_LRU // LRU_TILE, LRU_TILE, LRU_TILE)


def kernel(x, ffn1_pre_g, ffn1_w_gu, ffn1_w_down, ffn1_post_g, mix_pre_g, w_in, conv_w, conv_b, w_rg, b_rg, w_ig, b_ig, lru_lambda, sinks, g_lru_out, g_attn_out, w_o, mix_post_g, ffn2_pre_g, ffn2_w_gu, ffn2_w_down, ffn2_post_g):
    bsz, s, d = x.shape
    depth = ffn1_pre_g.shape[0]
    for l in range(depth):
        x2 = _ffn(x.reshape(bsz * s, d), ffn1_pre_g[l][None], ffn1_w_gu[l].astype(BF16),
                  ffn1_w_down[l].astype(BF16), ffn1_post_g[l][None])
        x = _mixer(x2.reshape(bsz, s, d), sinks[l], mix_pre_g[l][None], w_in[l].astype(BF16),
                   conv_w[l], conv_b[l][None],
                   _block_diag_tiles(w_rg[l]).astype(BF16), b_rg[l][None],
                   _block_diag_tiles(w_ig[l]).astype(BF16), b_ig[l][None],
                   lru_lambda[l][None], g_lru_out[l][None], g_attn_out[l][None],
                   w_o[l].astype(BF16), mix_post_g[l][None])
        x2 = _ffn(x.reshape(bsz * s, d), ffn2_pre_g[l][None], ffn2_w_gu[l].astype(BF16),
                  ffn2_w_down[l].astype(BF16), ffn2_post_g[l][None])
        x = x2.reshape(bsz, s, d)
    return x
```

```python
import jax
import jax.numpy as jnp
from jax import lax
from jax.experimental import pallas as pl
from jax.experimental.pallas import tpu as pltpu

D_MODEL = 1024
D_LRU = 512
LRU_BLOCKS = 8
LRU_BLOCK = 64
CONV_WIDTH = 4
LRU_C = 8.0
N_HEADS = 8
N_KV_HEADS = 2
GQA_GROUP = 4
HEAD_DIM = 64
D_ATTN = 512
WINDOW = 128
BLOCK_Q = 128
D_FF = 2816
RMS_EPS = 1e-6
MASK_VALUE = -1e30

O_GATE = D_LRU
O_Q = 2 * D_LRU
O_K = O_Q + D_ATTN
O_V = O_K + N_KV_HEADS * HEAD_DIM
D_IN = O_V + N_KV_HEADS * HEAD_DIM

LANES = 128
SUBLANES = 8
LRU_TILE = 256
LRU_SLABS = D_LRU // LANES

FFN_TM = 512
FFN_FC = 256
MIX_T = 512
MIX_NG = MIX_T // SUBLANES
VMEM_LIMIT = 56 * 1024 * 1024

F32 = jnp.float32
BF16 = jnp.bfloat16


def _rms(x, g):
    ms = jnp.mean(x * x, axis=-1, keepdims=True)
    return (x * lax.rsqrt(ms + RMS_EPS)) * g


def _ffn_kernel(x_ref, pre_g_ref, w_gu_ref, w_down_ref, post_g_ref, o_ref, acc_ref):
    x = x_ref[...]
    h = _rms(x, pre_g_ref[...]).astype(BF16)
    for c in range(D_FF // FFN_FC):
        lo = c * FFN_FC
        g = jnp.dot(h, w_gu_ref[:, lo:lo + FFN_FC], preferred_element_type=F32)
        u = jnp.dot(h, w_gu_ref[:, D_FF + lo:D_FF + lo + FFN_FC], preferred_element_type=F32)
        a = ((g * jax.nn.sigmoid(g)) * u).astype(BF16)
        d = jnp.dot(a, w_down_ref[lo:lo + FFN_FC, :], preferred_element_type=F32)
        if c == 0:
            acc_ref[...] = d
        else:
            acc_ref[...] += d
    o_ref[...] = x + 0.5 * _rms(acc_ref[...], post_g_ref[...])


def _ffn(x2d, pre_g, w_gu, w_down, post_g):
    n = x2d.shape[0]
    const = lambda i: (0, 0)
    return pl.pallas_call(
        _ffn_kernel,
        name="ffn",
        grid=(n // FFN_TM,),
        in_specs=[
            pl.BlockSpec((FFN_TM, D_MODEL), lambda i: (i, 0)),
            pl.BlockSpec((1, D_MODEL), const),
            pl.BlockSpec((D_MODEL, 2 * D_FF), const, pipeline_mode=pl.Buffered(1)),
            pl.BlockSpec((D_FF, D_MODEL), const, pipeline_mode=pl.Buffered(1)),
            pl.BlockSpec((1, D_MODEL), const),
        ],
        out_specs=pl.BlockSpec((FFN_TM, D_MODEL), lambda i: (i, 0)),
        out_shape=jax.ShapeDtypeStruct((n, D_MODEL), F32),
        scratch_shapes=[pltpu.VMEM((FFN_TM, D_MODEL), F32)],
        compiler_params=pltpu.CompilerParams(
            dimension_semantics=("arbitrary",), vmem_limit_bytes=VMEM_LIMIT),
    )(x2d, pre_g, w_gu, w_down, post_g)


def _log_sigmoid(x):
    return jnp.minimum(x, 0.0) - jnp.log1p(jnp.exp(-jnp.abs(x)))


def _mixer_kernel(sinks_ref, x_ref, pre_g_ref, w_in_ref, conv_w_ref, conv_b_ref,
                  w_rg_ref, b_rg_ref, w_ig_ref, b_ig_ref, lam_ref,
                  g_lru_ref, g_attn_ref, w_o_ref, post_g_ref,
                  o_ref,
                  xl_ref, xtail_ref, h_ref, ga_ref, gh_ref, c_ref, hcarry_ref, kbuf_ref, vaug_ref,
                  yattn_ref, cap_ref, sink_ref):
    t = pl.program_id(1)
    T, NG = MIX_T, MIX_NG
    n_kv = N_KV_HEADS * HEAD_DIM
    rows4 = GQA_GROUP * BLOCK_Q
    cur = lax.rem(t, 2)
    prev = 1 - cur

    @pl.when(t == 0)
    def _():
        xtail_ref[...] = jnp.zeros_like(xtail_ref)
        hcarry_ref[...] = jnp.zeros_like(hcarry_ref)
        kbuf_ref[1, T - BLOCK_Q:T, :] = jnp.zeros((BLOCK_Q, n_kv), BF16)
        for g in range(N_KV_HEADS):
            vaug_ref[N_KV_HEADS + g, T - BLOCK_Q:T, 0:LANES] = jnp.zeros((BLOCK_Q, LANES), BF16)
        vaug_ref[:, :, LANES:2 * LANES] = jnp.ones((2 * N_KV_HEADS, T, LANES), BF16)
        qi = lax.broadcasted_iota(jnp.int32, (rows4, 2 * BLOCK_Q), 0) & (BLOCK_Q - 1)
        kj = lax.broadcasted_iota(jnp.int32, (rows4, 2 * BLOCK_Q), 1)
        rel = qi + BLOCK_Q - kj
        band = (rel >= 0) & (rel < WINDOW)
        cap_ref[0] = jnp.where(band, jnp.inf, MASK_VALUE)
        cap_ref[1] = jnp.where(band & (kj >= BLOCK_Q), jnp.inf, MASK_VALUE)
        head_of_row = lax.broadcasted_iota(jnp.int32, (rows4, LANES), 0) // BLOCK_Q
        for g in range(N_KV_HEADS):
            sink = jnp.full((rows4, LANES), sinks_ref[g * GQA_GROUP + GQA_GROUP - 1], F32)
            for hh in range(GQA_GROUP - 1):
                sink = jnp.where(head_of_row == hh, sinks_ref[g * GQA_GROUP + hh], sink)
            sink_ref[g] = sink

    @pl.when(t == 1)
    def _():
        cap_ref[1] = cap_ref[0]

    x = x_ref[...]
    hn = _rms(x, pre_g_ref[...]).astype(BF16)
    proj = jnp.dot(hn, w_in_ref[...], preferred_element_type=F32)

    gate = proj[:, O_GATE:O_GATE + D_LRU]
    for j in range(LRU_SLABS):
        xl_ref[j, 0:SUBLANES, :] = xtail_ref[prev * LRU_SLABS + j]
        xl_ref[j, SUBLANES:SUBLANES + T, :] = proj[:, j * LANES:(j + 1) * LANES]
        xtail_ref[cur * LRU_SLABS + j] = proj[T - SUBLANES:T, j * LANES:(j + 1) * LANES]

    def x_rows(r):
        return jnp.concatenate(
            [xl_ref[j, pl.ds(SUBLANES + r, NG, stride=SUBLANES), :] for j in range(LRU_SLABS)],
            axis=-1)

    xs = {r: x_rows(r) for r in range(1 - CONV_WIDTH, SUBLANES)}
    cw = conv_w_ref[...]
    cb = conv_b_ref[...]
    xc_blocks = []
    for r in range(SUBLANES):
        acc = cb + cw[CONV_WIDTH - 1:CONV_WIDTH, :] * xs[r]
        for j in range(1, CONV_WIDTH):
            k = CONV_WIDTH - 1 - j
            acc = acc + cw[k:k + 1, :] * xs[r - j]
        xc_blocks.append(acc)
    xc = jnp.concatenate(xc_blocks, axis=0)

    xcb = xc.astype(BF16)
    r_parts, i_parts = [], []
    for b in range(D_LRU // LRU_TILE):
        xt = xcb[:, b * LRU_TILE:(b + 1) * LRU_TILE]
        r_parts.append(jnp.dot(xt, w_rg_ref[b], preferred_element_type=F32))
        i_parts.append(jnp.dot(xt, w_ig_ref[b], preferred_element_type=F32))
    r = jax.nn.sigmoid(jnp.concatenate(r_parts, axis=-1) + b_rg_ref[...])
    ig = jax.nn.sigmoid(jnp.concatenate(i_parts, axis=-1) + b_ig_ref[...])
    log_a = (LRU_C * r) * _log_sigmoid(lam_ref[...])
    a = jnp.exp(log_a)
    mult = jnp.sqrt(-jnp.tanh(log_a) * (a * a + 1.0))
    u = mult * (ig * xc)

    hs, As = [], []
    for r in range(SUBLANES):
        a_r = a[r * NG:(r + 1) * NG, :]
        u_r = u[r * NG:(r + 1) * NG, :]
        if r == 0:
            h_r, A_r = u_r, a_r
        else:
            h_r = a_r * hs[-1] + u_r
            A_r = a_r * As[-1]
        hs.append(h_r)
        As.append(A_r)
    ga_ref[...] = As[-1]
    gh_ref[...] = hs[-1]
    carry = hcarry_ref[0:1, :]
    for g in range(NG):
        c_ref[g:g + 1, :] = carry
        carry = ga_ref[g:g + 1, :] * carry + gh_ref[g:g + 1, :]
    hcarry_ref[0:1, :] = carry
    c_in = c_ref[...]
    for r in range(SUBLANES):
        hf = hs[r] + As[r] * c_in
        for j in range(LRU_SLABS):
            h_ref[j, pl.ds(r, NG, stride=SUBLANES), :] = hf[:, j * LANES:(j + 1) * LANES]
    h_tok = jnp.concatenate([h_ref[j] for j in range(LRU_SLABS)], axis=-1)

    y_lru = h_tok * jax.nn.gelu(gate, approximate=True)
    yl = _rms(y_lru, g_lru_ref[...])

    scale = HEAD_DIM ** -0.5
    q = (proj[:, O_Q:O_Q + D_ATTN] * scale).astype(BF16)
    kbuf_ref[cur] = proj[:, O_K:O_K + n_kv].astype(BF16)
    v2 = proj[:, O_V:O_V + n_kv]
    v2r = pltpu.roll(v2, HEAD_DIM, 1)
    lane_v = lax.broadcasted_iota(jnp.int32, (T, LANES), 1)
    vaug_ref[cur * N_KV_HEADS, :, 0:LANES] = jnp.where(lane_v < HEAD_DIM, v2, v2r).astype(BF16)
    vaug_ref[cur * N_KV_HEADS + 1, :, 0:LANES] = jnp.where(lane_v < HEAD_DIM, v2r, v2).astype(BF16)

    def band_rows(ref, i_cur, i_prev, qb):
        if qb == 0:
            return jnp.concatenate([ref[i_prev, T - BLOCK_Q:T, :], ref[i_cur, 0:BLOCK_Q, :]], axis=0)
        return ref[i_cur, (qb - 1) * BLOCK_Q:(qb + 1) * BLOCK_Q, :]

    n_qb = T // BLOCK_Q
    s_list, sink_list = [], []
    for qb in range(n_qb):
        qblk = q[qb * BLOCK_Q:(qb + 1) * BLOCK_Q, :]
        kband = band_rows(kbuf_ref, cur, prev, qb)
        cap = cap_ref[1 if qb == 0 else 0]
        for g in range(N_KV_HEADS):
            heads = [g * GQA_GROUP + hh for hh in range(GQA_GROUP)]
            q4 = jnp.concatenate([qblk[:, h * HEAD_DIM:(h + 1) * HEAD_DIM] for h in heads], axis=0)
            kg = kband[:, g * HEAD_DIM:(g + 1) * HEAD_DIM]
            s = lax.dot_general(q4, kg, (((1,), (1,)), ((), ())), preferred_element_type=F32)
            s_list.append(jnp.minimum(s, cap))
            sink_list.append(sink_ref[g])
    s_all = jnp.concatenate(s_list, axis=0)
    sink_all = jnp.concatenate(sink_list, axis=0)
    row_max = jnp.max(s_all, axis=-1, keepdims=True)
    mx = jnp.maximum(jnp.broadcast_to(row_max, sink_all.shape), sink_all)
    p_all = jnp.exp(s_all - jnp.concatenate([mx, mx], axis=-1)).astype(BF16)
    e_sink = jnp.exp(sink_all - mx)
    lane_o = lax.broadcasted_iota(jnp.int32, (BLOCK_Q, LANES), 1)
    for qb in range(n_qb):
        for g in range(N_KV_HEADS):
            i0 = (qb * N_KV_HEADS + g) * rows4
            vband = band_rows(vaug_ref, cur * N_KV_HEADS + g, prev * N_KV_HEADS + g, qb)
            pv = jnp.dot(p_all[i0:i0 + rows4, :], vband,
                         preferred_element_type=F32)
            o2 = pv[:, 0:LANES] / (pv[:, LANES:2 * LANES] + e_sink[i0:i0 + rows4, :])
            for pair in range(GQA_GROUP // 2):
                even = o2[(2 * pair) * BLOCK_Q:(2 * pair + 1) * BLOCK_Q, :]
                odd = o2[(2 * pair + 1) * BLOCK_Q:(2 * pair + 2) * BLOCK_Q, :]
                col = (g * (GQA_GROUP // 2) + pair) * LANES
                yattn_ref[qb * BLOCK_Q:(qb + 1) * BLOCK_Q, col:col + LANES] = (
                    jnp.where(lane_o < HEAD_DIM, even, odd))

    ya = _rms(yattn_ref[...], g_attn_ref[...])
    y = jnp.concatenate([yl, ya], axis=-1).astype(BF16)
    out = jnp.dot(y, w_o_ref[...], preferred_element_type=F32)
    o_ref[...] = x + _rms(out, post_g_ref[...])


def _mixer(x, sinks, pre_g, w_in, conv_w, conv_b, w_rg, b_rg, w_ig, b_ig, lam,
           g_lru, g_attn, w_o, post_g):
    bsz, s, _ = x.shape
    T, NG = MIX_T, MIX_NG
    c2 = lambda b, t, *_: (0, 0)
    c3 = lambda b, t, *_: (0, 0, 0)
    n_kv = N_KV_HEADS * HEAD_DIM
    grid_spec = pltpu.PrefetchScalarGridSpec(
        num_scalar_prefetch=1,
        grid=(bsz, s // T),
        in_specs=[
            pl.BlockSpec((None, T, D_MODEL), lambda b, t, *_: (b, t, 0)),
            pl.BlockSpec((1, D_MODEL), c2),
            pl.BlockSpec((D_MODEL, D_IN), c2),
            pl.BlockSpec((CONV_WIDTH, D_LRU), c2),
            pl.BlockSpec((1, D_LRU), c2),
            pl.BlockSpec((D_LRU // LRU_TILE, LRU_TILE, LRU_TILE), c3),
            pl.BlockSpec((1, D_LRU), c2),
            pl.BlockSpec((D_LRU // LRU_TILE, LRU_TILE, LRU_TILE), c3),
            pl.BlockSpec((1, D_LRU), c2),
            pl.BlockSpec((1, D_LRU), c2),
            pl.BlockSpec((1, D_LRU), c2),
            pl.BlockSpec((1, D_ATTN), c2),
            pl.BlockSpec((D_MODEL, D_MODEL), c2),
            pl.BlockSpec((1, D_MODEL), c2),
        ],
        out_specs=pl.BlockSpec((None, T, D_MODEL), lambda b, t, *_: (b, t, 0)),
        scratch_shapes=[
            pltpu.VMEM((LRU_SLABS, SUBLANES + T, LANES), F32),
            pltpu.VMEM((2 * LRU_SLABS, SUBLANES, LANES), F32),
            pltpu.VMEM((LRU_SLABS, T, LANES), F32),
            pltpu.VMEM((NG, D_LRU), F32),
            pltpu.VMEM((NG, D_LRU), F32),
            pltpu.VMEM((NG, D_LRU), F32),
            pltpu.VMEM((SUBLANES, D_LRU), F32),
            pltpu.VMEM((2, T, n_kv), BF16),
            pltpu.VMEM((2 * N_KV_HEADS, T, 2 * LANES), BF16),
            pltpu.VMEM((T, D_ATTN), F32),
            pltpu.VMEM((2, GQA_GROUP * BLOCK_Q, 2 * BLOCK_Q), F32),
            pltpu.VMEM((N_KV_HEADS, GQA_GROUP * BLOCK_Q, LANES), F32),
        ],
    )
    return pl.pallas_call(
        _mixer_kernel,
        name="mixer",
        grid_spec=grid_spec,
        out_shape=jax.ShapeDtypeStruct(x.shape, F32),
        compiler_params=pltpu.CompilerParams(
            dimension_semantics=("arbitrary", "arbitrary"), vmem_limit_bytes=VMEM_LIMIT),
    )(sinks, x, pre_g, w_in, conv_w, conv_b, w_rg, b_rg, w_ig, b_ig, lam,
      g_lru, g_attn, w_o, post_g)


def _block_diag_tiles(w):
    per = LRU_TILE // LRU_BLOCK
    w4 = w.reshape(D_LRU // LRU_TILE, per, LRU_BLOCK, LRU_BLOCK)
    eye = jnp.eye(per, dtype=w.dtype)
    t = jnp.einsum("tpij,pq->tpiqj", w4, eye)
    return t.reshape(D_LRU // LRU_TILE, LRU_TILE, LRU_TILE)


def kernel(x, ffn1_pre_g, ffn1_w_gu, ffn1_w_down, ffn1_post_g, mix_pre_g, w_in, conv_w, conv_b, w_rg, b_rg, w_ig, b_ig, lru_lambda, sinks, g_lru_out, g_attn_out, w_o, mix_post_g, ffn2_pre_g, ffn2_w_gu, ffn2_w_down, ffn2_post_g):
    bsz, s, d = x.shape
    depth = ffn1_pre_g.shape[0]
    for l in range(depth):
        x2 = _ffn(x.reshape(bsz * s, d), ffn1_pre_g[l][None], ffn1_w_gu[l].astype(BF16),
                  ffn1_w_down[l].astype(BF16), ffn1_post_g[l][None])
        x = _mixer(x2.reshape(bsz, s, d), sinks[l], mix_pre_g[l][None], w_in[l].astype(BF16),
                   conv_w[l], conv_b[l][None],
                   _block_diag_tiles(w_rg[l]).astype(BF16), b_rg[l][None],
                   _block_diag_tiles(w_ig[l]).astype(BF16), b_ig[l][None],
                   lru_lambda[l][None], g_lru_out[l][None], g_attn_out[l][None],
                   w_o[l].astype(BF16), mix_post_g[l][None])
        x2 = _ffn(x.reshape(bsz * s, d), ffn2_pre_g[l][None], ffn2_w_gu[l].astype(BF16),
                  ffn2_w_down[l].astype(BF16), ffn2_post_g[l][None])
        x = x2.reshape(bsz, s, d)
    return x
```
